```python
import math
import jax, jax.numpy as jnp
from jax import lax
import numpy as np

D_MODEL = 1024
BATCH = 8
SEQ = 2048
DEPTH = 4

D_INNER = 2 * D_MODEL
CONV_WIDTH = 4
EPS = 1e-6
LRU_WIDTH = D_INNER // 4
LRU_HEADS = 8
LRU_HEAD_DIM = LRU_WIDTH // LRU_HEADS
LRU_C = 8.0
HG_WIDTH = D_INNER // 4
HG_HEAD_DIM = 128
HG_HEADS = HG_WIDTH // HG_HEAD_DIM
HG_CHUNK = 64
SSD_WIDTH = D_INNER // 2
SSD_HEAD_DIM = 64
SSD_HEADS = SSD_WIDTH // SSD_HEAD_DIM
SSD_GROUPS = 2
SSD_STATE = 128
SSD_CHUNK = 128
SSD_CONV_DIM = SSD_WIDTH + 2 * SSD_GROUPS * SSD_STATE
SPLIT_SIZES = (LRU_WIDTH, LRU_WIDTH,
               HG_WIDTH, HG_WIDTH, HG_WIDTH, HG_WIDTH,
               SSD_WIDTH, SSD_CONV_DIM, SSD_HEADS)
N_IN = sum(SPLIT_SIZES)

kernel_name = "hybrid_rglru_hgrn2_ssd_parallel_heads"


def _split_points():
    return [int(v) for v in np.cumsum(SPLIT_SIZES)[:-1]]


def rmsnorm(x, w):
    xf = x.astype(jnp.float32)
    inv = lax.rsqrt(jnp.mean(xf * xf, axis=-1, keepdims=True) + EPS)
    return (xf * inv).astype(x.dtype) * w


def causal_conv(x, w, b):
    K = w.shape[0]
    S = x.shape[1]
    xp = jnp.pad(x, ((0, 0), (K - 1, 0), (0, 0)))
    out = b
    for k in range(K):
        out = out + xp[:, k:k + S] * w[k]
    return out


def rg_lru(x, wa, ba, wx, bx, lam):
    B, S, _ = x.shape
    xh = x.reshape(B, S, LRU_HEADS, LRU_HEAD_DIM)
    r = jax.nn.sigmoid(jnp.einsum('bshi,hij->bshj', xh, wa) + ba).reshape(B, S, LRU_WIDTH)
    i = jax.nn.sigmoid(jnp.einsum('bshi,hij->bshj', xh, wx) + bx).reshape(B, S, LRU_WIDTH)
    log_a = -LRU_C * r * jax.nn.softplus(-lam)
    a = jnp.exp(log_a)
    mult = jnp.sqrt(-jnp.expm1(2.0 * log_a))
    b = mult * (i * x)

    def combine(left, right):
        a1, b1 = left
        a2, b2 = right
        return a1 * a2, a2 * b1 + b2

    _, h = lax.associative_scan(combine, (a, b), axis=1)
    return h


def hgrn2_chunked(q, k, v, log_f):
    dtype = v.dtype
    q, k, v, log_f = (t.astype(jnp.float32) for t in (q, k, v, log_f))
    B, S, H, DK = q.shape
    DV = v.shape[-1]
    n = S // HG_CHUNK

    def to_chunks(t):
        return t.reshape(B, n, HG_CHUNK, H, t.shape[-1]).transpose(1, 0, 3, 2, 4)

    qc, kc, vc, lc = (to_chunks(t) for t in (q, k, v, log_f))
    causal = jnp.tril(jnp.ones((HG_CHUNK, HG_CHUNK), bool))[:, :, None]

    def step(state, inp):
        qi, ki, vi, li = inp
        cum = jnp.cumsum(li, axis=2)
        diff = cum[:, :, :, None, :] - cum[:, :, None, :, :]
        decay = jnp.where(causal, jnp.exp(jnp.where(causal, diff, 0.0)), 0.0)
        scores = jnp.einsum('bhtd,bhsd,bhtsd->bhts', qi, ki, decay)
        o = (jnp.einsum('bhts,bhsv->bhtv', scores, vi)
             + jnp.einsum('bhtd,bhdv->bhtv', qi * jnp.exp(cum), state))
        last = cum[:, :, -1:, :]
        state = (state * jnp.exp(last[:, :, 0, :, None])
                 + jnp.einsum('bhsd,bhsv->bhdv', ki * jnp.exp(last - cum), vi))
        return state, o

    s0 = jnp.zeros((B, H, DK, DV), jnp.float32)
    _, o = lax.scan(step, s0, (qc, kc, vc, lc))
    return o.transpose(1, 0, 3, 2, 4).reshape(B, S, H, DV).astype(dtype)


def ssd_chunked(x, dt, A, Bm, Cm):
    dtype = x.dtype
    x, dt, A, Bm, Cm = (t.astype(jnp.float32) for t in (x, dt, A, Bm, Cm))
    B, S, H, P = x.shape
    G, N = Bm.shape[2], Bm.shape[3]
    J = H // G
    C = SSD_CHUNK
    n = S // C
    xdt = (x * dt[..., None]).reshape(B, n, C, G, J, P).transpose(1, 0, 2, 3, 4, 5)
    dA = (dt * A).reshape(B, n, C, G, J).transpose(1, 0, 2, 3, 4)
    Bc = Bm.reshape(B, n, C, G, N).transpose(1, 0, 2, 3, 4)
    Cc = Cm.reshape(B, n, C, G, N).transpose(1, 0, 2, 3, 4)
    causal = jnp.tril(jnp.ones((C, C), bool))[:, :, None, None]

    def step(state, inp):
        xi, ai, bi, ci = inp
        cum = jnp.cumsum(ai, axis=1)
        diff = cum[:, :, None] - cum[:, None, :]
        L = jnp.where(causal, jnp.exp(jnp.where(causal, diff, 0.0)), 0.0)
        cb = jnp.einsum('btgn,bsgn->btsg', ci, bi)
        y = jnp.einsum('btsg,btsgj,bsgjp->btgjp', cb, L, xi)
        y = y + jnp.einsum('btgn,bgjpn,btgj->btgjp', ci, state, jnp.exp(cum))
        last = cum[:, -1]
        state = (state * jnp.exp(last)[..., None, None]
                 + jnp.einsum('bsgn,bsgj,bsgjp->bgjpn', bi, jnp.exp(last[:, None] - cum), xi))
        return state, y

    s0 = jnp.zeros((B, G, J, P, N), jnp.float32)
    _, y = lax.scan(step, s0, (xdt, dA, Bc, Cc))
    return y.transpose(1, 0, 2, 3, 4, 5).reshape(B, S, H, P).astype(dtype)


def setup_inputs(seed: int = 0) -> dict:
    key = jax.random.key(seed)
    ks = jax.random.split(key, 24)
    f32 = jnp.float32

    def nrm(k, shape, scale):
        return jax.random.normal(k, shape, f32) * scale

    x = nrm(ks[0], (BATCH, SEQ, D_MODEL), 1.0)
    c = nrm(ks[1], (BATCH, D_MODEL), 1.0)
    norm_w = 1.0 + nrm(ks[2], (DEPTH, D_MODEL), 0.01)
    w_ada = nrm(ks[3], (DEPTH, D_MODEL, 3 * D_MODEL), 0.5 * D_MODEL ** -0.5)
    b_ada = nrm(ks[4], (DEPTH, 3 * D_MODEL), 0.01)
    w_in = nrm(ks[5], (DEPTH, D_MODEL, N_IN), D_MODEL ** -0.5)
    lru_conv_w = nrm(ks[6], (DEPTH, CONV_WIDTH, LRU_WIDTH), CONV_WIDTH ** -0.5)
    lru_conv_b = nrm(ks[7], (DEPTH, LRU_WIDTH), 0.01)
    lru_wa = nrm(ks[8], (DEPTH, LRU_HEADS, LRU_HEAD_DIM, LRU_HEAD_DIM), LRU_HEAD_DIM ** -0.5)
    lru_ba = nrm(ks[9], (DEPTH, LRU_HEADS, LRU_HEAD_DIM), 0.01)
    lru_wx = nrm(ks[10], (DEPTH, LRU_HEADS, LRU_HEAD_DIM, LRU_HEAD_DIM), LRU_HEAD_DIM ** -0.5)
    lru_bx = nrm(ks[11], (DEPTH, LRU_HEADS, LRU_HEAD_DIM), 0.01)
    u = jax.random.uniform(ks[12], (DEPTH, LRU_WIDTH), f32, 0.9, 0.999) ** (1.0 / LRU_C)
    lru_lambda = jnp.log(u) - jnp.log1p(-u)
    hg_lb_logits = nrm(ks[13], (DEPTH, HG_WIDTH), 0.5)
    hg_norm_w = 1.0 + nrm(ks[14], (DEPTH, HG_WIDTH), 0.01)
    ssd_conv_w = nrm(ks[15], (DEPTH, CONV_WIDTH, SSD_CONV_DIM), CONV_WIDTH ** -0.5)
    ssd_conv_b = nrm(ks[16], (DEPTH, SSD_CONV_DIM), 0.01)
    dt0 = jnp.exp(jax.random.uniform(ks[17], (DEPTH, SSD_HEADS), f32, math.log(1e-3), math.log(1e-1)))
    ssd_dt_bias = dt0 + jnp.log(-jnp.expm1(-dt0))
    ssd_a_log = jnp.log(jax.random.uniform(ks[18], (DEPTH, SSD_HEADS), f32, 1.0, 16.0))
    ssd_d = 1.0 + nrm(ks[19], (DEPTH, SSD_HEADS), 0.01)
    ssd_norm_w = 1.0 + nrm(ks[20], (DEPTH, SSD_WIDTH), 0.01)
    w_out = nrm(ks[21], (DEPTH, D_INNER, D_MODEL), D_INNER ** -0.5)
    final_norm_w = 1.0 + nrm(ks[22], (D_MODEL,), 0.01)
    return {"x": x, "c": c, "norm_w": norm_w, "w_ada": w_ada, "b_ada": b_ada,
            "w_in": w_in, "lru_conv_w": lru_conv_w, "lru_conv_b": lru_conv_b,
            "lru_wa": lru_wa, "lru_ba": lru_ba, "lru_wx": lru_wx, "lru_bx": lru_bx,
            "lru_lambda": lru_lambda, "hg_lb_logits": hg_lb_logits, "hg_norm_w": hg_norm_w,
            "ssd_conv_w": ssd_conv_w, "ssd_conv_b": ssd_conv_b, "ssd_dt_bias": ssd_dt_bias,
            "ssd_a_log": ssd_a_log, "ssd_d": ssd_d, "ssd_norm_w": ssd_norm_w,
            "w_out": w_out, "final_norm_w": final_norm_w}


def reference(x, c, norm_w, w_ada, b_ada, w_in, lru_conv_w, lru_conv_b, lru_wa, lru_ba,
              lru_wx, lru_bx, lru_lambda, hg_lb_logits, hg_norm_w, ssd_conv_w, ssd_conv_b,
              ssd_dt_bias, ssd_a_log, ssd_d, ssd_norm_w, w_out, final_norm_w):
    B, S, _ = x.shape
    split_points = _split_points()
    cond = jax.nn.silu(c)
    p = jax.nn.softmax(hg_lb_logits.astype(jnp.float32), axis=0)
    lower_bounds = jnp.cumsum(p, axis=0) - p[0]

    for l in range(DEPTH):
        mod = cond @ w_ada[l] + b_ada[l]
        shift, scale, gate = jnp.split(mod, 3, axis=-1)
        h = rmsnorm(x, norm_w[l]) * (1.0 + scale[:, None]) + shift[:, None]
        u = h @ w_in[l]
        a_x, a_g, hg_q, hg_f, hg_i, hg_g, ssd_z, ssd_xbc, ssd_dt = jnp.split(u, split_points, axis=-1)

        xa = causal_conv(a_x, lru_conv_w[l], lru_conv_b[l])
        ya = rg_lru(xa, lru_wa[l], lru_ba[l], lru_wx[l], lru_bx[l], lru_lambda[l]) * jax.nn.silu(a_g)

        lb = lower_bounds[l]
        hf = hg_f.astype(jnp.float32)
        f = lb + (1.0 - lb) * jax.nn.sigmoid(hf)
        log_f = jnp.log(f)
        k_in = (1.0 - lb) * jax.nn.sigmoid(-hf)
        q_h = jax.nn.silu(hg_q).reshape(B, S, HG_HEADS, HG_HEAD_DIM)
        o_b = hgrn2_chunked(q_h, k_in.reshape(B, S, HG_HEADS, HG_HEAD_DIM),
                            hg_i.reshape(B, S, HG_HEADS, HG_HEAD_DIM),
                            log_f.reshape(B, S, HG_HEADS, HG_HEAD_DIM))
        yb = rmsnorm(o_b.reshape(B, S, HG_WIDTH), hg_norm_w[l]) * jax.nn.silu(hg_g)

        xbc = jax.nn.silu(causal_conv(ssd_xbc, ssd_conv_w[l], ssd_conv_b[l]))
        xs, Bm, Cm = jnp.split(xbc, [SSD_WIDTH, SSD_WIDTH + SSD_GROUPS * SSD_STATE], axis=-1)
        dt = jax.nn.softplus(ssd_dt + ssd_dt_bias[l])
        A = -jnp.exp(ssd_a_log[l])
        xs_h = xs.reshape(B, S, SSD_HEADS, SSD_HEAD_DIM)
        yc = ssd_chunked(xs_h, dt, A, Bm.reshape(B, S, SSD_GROUPS, SSD_STATE),
                         Cm.reshape(B, S, SSD_GROUPS, SSD_STATE))
        yc = (yc + ssd_d[l][:, None] * xs_h).reshape(B, S, SSD_WIDTH) * jax.nn.silu(ssd_z)
        yc = rmsnorm(yc.reshape(B, S, SSD_GROUPS, SSD_WIDTH // SSD_GROUPS),
                     ssd_norm_w[l].reshape(SSD_GROUPS, SSD_WIDTH // SSD_GROUPS)).reshape(B, S, SSD_WIDTH)

        y = jnp.concatenate([ya, yb, yc], axis=-1) @ w_out[l]
        x = x + gate[:, None] * y

    return rmsnorm(x, final_norm_w)
```

```python
import functools

import jax
import jax.numpy as jnp
from jax import lax
from jax.experimental import pallas as pl
from jax.experimental.pallas import tpu as pltpu

F32 = jnp.float32
BF16 = jnp.bfloat16

EPS = 1e-6
CONV_K = 4
LRU_W = 512
LRU_HEADS = 8
LRU_HD = 64
LRU_C = 8.0
HG_W = 512
HG_HD = 128
HG_HEADS = 4
HG_CHUNK = 64
SSD_W = 1024
SSD_P = 64
SSD_HEADS = 16
SSD_GROUPS = 2
SSD_N = 128
SSD_CHUNK = 128
SSD_XBC = SSD_W + 2 * SSD_GROUPS * SSD_N
LANES = 128
SUBLANES = 8
DT_PAD = LANES
OFF_A = 0
OFF_B = OFF_A + 2 * LRU_W
OFF_C = OFF_B + 4 * HG_W
N_MAIN = OFF_C + SSD_W + SSD_XBC
N_ALL = N_MAIN + DT_PAD
TILE_ROWS = 256
VMEM_LIMIT_BYTES = 56 * 1024 * 1024


def _sigmoid(x):
    return 0.5 * jnp.tanh(0.5 * x) + 0.5


def _silu(x):
    return x * _sigmoid(x)


def _softplus(x):
    return jnp.maximum(x, 0.0) + jnp.log1p(jnp.exp(-jnp.abs(x)))


def _rms(x):
    return x * lax.rsqrt(jnp.mean(x * x, axis=-1, keepdims=True) + EPS)


def _dot(a, b):
    return jnp.dot(a.astype(BF16), b.astype(BF16), preferred_element_type=F32)


def _dot_nt(a, b):
    return lax.dot_general(a.astype(BF16), b.astype(BF16), (((1,), (1,)), ((), ())),
                           preferred_element_type=F32)


def _dot_tn(a, b):
    return lax.dot_general(a.astype(BF16), b.astype(BF16), (((0,), (0,)), ((), ())),
                           preferred_element_type=F32)


def _split(x):
    hi = x.astype(BF16)
    lo = (x - hi.astype(F32)).astype(BF16)
    return hi, lo


def _dot_exact_rhs(t, x):
    hi, lo = _split(x)
    return (jnp.dot(t, hi, preferred_element_type=F32)
            + jnp.dot(t, lo, preferred_element_type=F32))


def _dot_exact_lhs(x, t):
    hi, lo = _split(x)
    return (jnp.dot(hi, t, preferred_element_type=F32)
            + jnp.dot(lo, t, preferred_element_type=F32))


def _chunk_tril(rows, chunk):
    t = lax.broadcasted_iota(jnp.int32, (rows, rows), 0)
    s = lax.broadcasted_iota(jnp.int32, (rows, rows), 1)
    keep = (s <= t) & ((t & -chunk) == (s & -chunk))
    return jnp.where(keep, 1.0, 0.0).astype(BF16)


def _causal_conv(pad_ref, w_ref, b_ref, rows):
    out = b_ref[...]
    for k in range(CONV_K):
        start = SUBLANES - (CONV_K - 1) + k
        out = out + pad_ref[start:start + rows, :] * w_ref[k:k + 1, :]
    return out


def _lru_scan(a, b, a_scr, b_scr, carry_ref, rows):
    width = a.shape[-1]
    rid = lax.broadcasted_iota(jnp.int32, (rows, width), 0) & (SUBLANES - 1)
    for s in (1, 2, 4):
        keep = rid >= s
        a_sh = jnp.where(keep, pltpu.roll(a, s, 0), 1.0)
        b_sh = jnp.where(keep, pltpu.roll(b, s, 0), 0.0)
        b = a * b_sh + b
        a = a * a_sh
    a_scr[...] = a
    b_scr[...] = b

    def body(g, hc):
        r0 = pl.multiple_of(g * SUBLANES, SUBLANES)
        hg = b_scr[pl.ds(r0, SUBLANES), :] + a_scr[pl.ds(r0, SUBLANES), :] * hc
        b_scr[pl.ds(r0, SUBLANES), :] = hg
        return jnp.broadcast_to(hg[SUBLANES - 1:SUBLANES, :], (SUBLANES, width))

    carry_ref[...] = lax.fori_loop(0, rows // SUBLANES, body, carry_ref[...])
    return b_scr[...]


def _layer_kernel(x_ref, mod_ref, normw_ref, wall_ref,
                  lcw_ref, lcb_ref, wg_ref, lba_ref, lbx_ref, lam_ref,
                  lb_ref, hgnw_ref,
                  scw_ref, scb_ref, dtb_ref, alog_ref, dexp_ref, snw_ref,
                  wout_ref, fnw_ref,
                  xo_ref,
                  conv_a, conv_c, lru_a, lru_b, lru_h, hg_st, ssd_st, o_scr, ys_scr, y_scr,
                  *, rows, final):
    j = pl.program_id(1)

    @pl.when(j == 0)
    def _reset_state():
        conv_a[0:SUBLANES, :] = jnp.zeros((SUBLANES, LRU_W), F32)
        conv_c[0:SUBLANES, :] = jnp.zeros((SUBLANES, SSD_XBC), F32)
        lru_h[...] = jnp.zeros_like(lru_h)
        hg_st[...] = jnp.zeros_like(hg_st)
        ssd_st[...] = jnp.zeros_like(ssd_st)

    x = x_ref[0]
    shift = mod_ref[0, 0:1, :]
    scale = mod_ref[0, 1:2, :]
    gate = mod_ref[0, 2:3, :]
    h = _rms(x) * normw_ref[...] * (1.0 + scale) + shift
    hb = h.astype(BF16)

    u_a = jnp.dot(hb, wall_ref[:, OFF_A:OFF_B], preferred_element_type=F32)
    conv_a[SUBLANES:SUBLANES + rows, :] = u_a[:, :LRU_W]
    xa = _causal_conv(conv_a, lcw_ref, lcb_ref, rows)
    conv_a[0:SUBLANES, :] = conv_a[rows:rows + SUBLANES, :]
    xab = xa.astype(BF16)
    half = LRU_W // 2
    pre0 = jnp.dot(xab[:, :half], wg_ref[0], preferred_element_type=F32)
    pre1 = jnp.dot(xab[:, half:], wg_ref[1], preferred_element_type=F32)
    r = _sigmoid(jnp.concatenate([pre0[:, :half], pre1[:, :half]], axis=-1) + lba_ref[...])
    i = _sigmoid(jnp.concatenate([pre0[:, half:], pre1[:, half:]], axis=-1) + lbx_ref[...])
    log_a = (-LRU_C * _softplus(-lam_ref[...])) * r
    a = jnp.exp(log_a)
    mult = jnp.sqrt(1.0 - a * a)
    hs = _lru_scan(a, mult * (i * xa), lru_a, lru_b, lru_h, rows)
    y_scr[:, 0:LRU_W] = (hs * _silu(u_a[:, LRU_W:])).astype(BF16)

    u_b = jnp.dot(hb, wall_ref[:, OFF_B:OFF_C], preferred_element_type=F32)
    lb = lb_ref[...]
    sig = _sigmoid(u_b[:, HG_W:2 * HG_W])
    g_log = jnp.log(lb + (1.0 - lb) * sig)
    k_in = (1.0 - lb) * (1.0 - sig)
    q_in = _silu(u_b[:, 0:HG_W])
    v_in = u_b[:, 2 * HG_W:3 * HG_W]
    cum = _dot_exact_rhs(_chunk_tril(rows, HG_CHUNK), g_log)
    ct = lax.broadcasted_iota(jnp.int32, (HG_CHUNK, HG_CHUNK), 0)
    cs = lax.broadcasted_iota(jnp.int32, (HG_CHUNK, HG_CHUNK), 1)
    causal_h = cs <= ct
    mid = HG_CHUNK // 2 - 1
    for c in range(rows // HG_CHUNK):
        r0 = c * HG_CHUNK
        cum_c = cum[r0:r0 + HG_CHUNK, :]
        ref_row = cum_c[mid:mid + 1, :]
        last = cum_c[HG_CHUNK - 1:HG_CHUNK, :]
        q_hat = q_in[r0:r0 + HG_CHUNK, :] * jnp.exp(cum_c - ref_row)
        k_hat = k_in[r0:r0 + HG_CHUNK, :] * jnp.exp(ref_row - cum_c)
        q_st = q_hat * jnp.exp(ref_row)
        k_st = k_hat * jnp.exp(last - ref_row)
        dec = jnp.exp(last)
        v_c = v_in[r0:r0 + HG_CHUNK, :]
        for hd in range(HG_HEADS):
            ls = slice(hd * HG_HD, (hd + 1) * HG_HD)
            sc = jnp.where(causal_h, _dot_nt(q_hat[:, ls], k_hat[:, ls]), 0.0)
            st = hg_st[hd]
            o_scr[r0:r0 + HG_CHUNK, ls] = _dot(sc, v_c[:, ls]) + _dot_nt(q_st[:, ls], st)
            hg_st[hd] = st * dec[:, ls] + _dot_tn(v_c[:, ls], k_st[:, ls])
    yb = _rms(o_scr[...]) * hgnw_ref[...] * _silu(u_b[:, 3 * HG_W:4 * HG_W])
    y_scr[:, LRU_W:LRU_W + HG_W] = yb.astype(BF16)

    u_c = jnp.dot(hb, wall_ref[:, OFF_C:N_ALL], preferred_element_type=F32)
    z_in = u_c[:, 0:SSD_W]
    conv_c[SUBLANES:SUBLANES + rows, :] = u_c[:, SSD_W:SSD_W + SSD_XBC]
    xbc = _silu(_causal_conv(conv_c, scw_ref, scb_ref, rows))
    conv_c[0:SUBLANES, :] = conv_c[rows:rows + SUBLANES, :]
    xs = xbc[:, 0:SSD_W]
    b_m = xbc[:, SSD_W:SSD_W + SSD_GROUPS * SSD_N]
    c_m = xbc[:, SSD_W + SSD_GROUPS * SSD_N:]
    dt = _softplus(u_c[:, SSD_W + SSD_XBC:] + dtb_ref[...])
    d_a = dt * (-jnp.exp(alog_ref[...]))
    cum_s = _dot_exact_rhs(_chunk_tril(rows, SSD_CHUNK), d_a)
    eh = lax.broadcasted_iota(jnp.int32, (DT_PAD, SSD_W), 0)
    el = lax.broadcasted_iota(jnp.int32, (DT_PAD, SSD_W), 1)
    expand = jnp.where((el & -SSD_P) == eh * SSD_P, 1.0, 0.0).astype(BF16)
    dt_e = _dot_exact_lhs(dt, expand)
    cum_e = _dot_exact_lhs(cum_s, expand)
    xdt = xs * dt_e
    ecum = jnp.exp(cum_e)
    lane = lax.broadcasted_iota(jnp.int32, (SSD_CHUNK, LANES), 1)
    low_half = lane < SSD_P
    st_ = lax.broadcasted_iota(jnp.int32, (SSD_CHUNK, SSD_CHUNK), 0)
    ss_ = lax.broadcasted_iota(jnp.int32, (SSD_CHUNK, SSD_CHUNK), 1)
    causal_s = ss_ <= st_
    gw = SSD_W // SSD_GROUPS
    for c in range(rows // SSD_CHUNK):
        r0 = c * SSD_CHUNK
        rs = slice(r0, r0 + SSD_CHUNK)
        cum_c = cum_s[rs, :]
        cum_t = cum_c.T
        last_e = cum_e[r0 + SSD_CHUNK - 1:r0 + SSD_CHUNK, :]
        xw = xdt[rs, :] * jnp.exp(last_e - cum_e[rs, :])
        dec_e = jnp.exp(last_e)
        for g in range(SSD_GROUPS):
            gl = slice(g * gw, (g + 1) * gw)
            b_g = b_m[rs, g * SSD_N:(g + 1) * SSD_N]
            c_g = c_m[rs, g * SSD_N:(g + 1) * SSD_N]
            cb = _dot_nt(c_g, b_g)
            st = ssd_st[:, gl]
            y_inter = _dot(c_g, st) * ecum[rs, gl]
            for p in range(gw // LANES):
                h0 = g * (SSD_HEADS // SSD_GROUPS) + 2 * p
                l0 = g * gw + p * LANES
                mats = []
                for hh in (h0, h0 + 1):
                    diff = cum_c[:, hh:hh + 1] - cum_t[hh:hh + 1, :]
                    mats.append(jnp.where(causal_s, cb * jnp.exp(diff), 0.0).astype(BF16))
                xb = xdt[rs, l0:l0 + LANES]
                x2 = jnp.concatenate([jnp.where(low_half, xb, 0.0),
                                      jnp.where(low_half, 0.0, xb)], axis=0).astype(BF16)
                y_pair = jnp.dot(jnp.concatenate(mats, axis=-1), x2, preferred_element_type=F32)
                ys_scr[rs, l0:l0 + LANES] = y_pair + y_inter[:, p * LANES:(p + 1) * LANES]
            ssd_st[:, gl] = st * dec_e[:, gl] + _dot_tn(b_g, xw[:, gl])
    yc = (ys_scr[...] + dexp_ref[...] * xs) * _silu(z_in)
    for g in range(SSD_GROUPS):
        gl = slice(g * gw, (g + 1) * gw)
        y_scr[:, LRU_W + HG_W + g * gw:LRU_W + HG_W + (g + 1) * gw] = (
            _rms(yc[:, gl]) * snw_ref[:, gl]).astype(BF16)

    xn = x + gate * jnp.dot(y_scr[...], wout_ref[...], preferred_element_type=F32)
    if final:
        xn = _rms(xn) * fnw_ref[...]
    xo_ref[0] = xn


def _const_spec(shape):
    nd = len(shape)
    return pl.BlockSpec(shape, lambda b, j: (0,) * nd)


def _layer_call(x, mod, params, *, final):
    bsz, seq, d = x.shape
    rows = min(TILE_ROWS, seq)
    assert seq % rows == 0 and rows % SSD_CHUNK == 0 and rows % HG_CHUNK == 0
    in_specs = [
        pl.BlockSpec((1, rows, d), lambda b, j: (b, j, 0)),
        pl.BlockSpec((1, 3, d), lambda b, j: (b, 0, 0)),
    ] + [_const_spec(p.shape) for p in params]
    scratch = [
        pltpu.VMEM((rows + SUBLANES, LRU_W), F32),
        pltpu.VMEM((rows + SUBLANES, SSD_XBC), F32),
        pltpu.VMEM((rows, LRU_W), F32),
        pltpu.VMEM((rows, LRU_W), F32),
        pltpu.VMEM((SUBLANES, LRU_W), F32),
        pltpu.VMEM((HG_HEADS, HG_HD, HG_HD), F32),
        pltpu.VMEM((SSD_N, SSD_W), F32),
        pltpu.VMEM((rows, HG_W), F32),
        pltpu.VMEM((rows, SSD_W), F32),
        pltpu.VMEM((rows, 2 * d), BF16),
    ]
    return pl.pallas_call(
        functools.partial(_layer_kernel, rows=rows, final=final),
        grid=(bsz, seq // rows),
        in_specs=in_specs,
        out_specs=pl.BlockSpec((1, rows, d), lambda b, j: (b, j, 0)),
        out_shape=jax.ShapeDtypeStruct(x.shape, x.dtype),
        scratch_shapes=scratch,
        compiler_params=pltpu.CompilerParams(
            dimension_semantics=("arbitrary", "arbitrary"),
            vmem_limit_bytes=VMEM_LIMIT_BYTES),
        name="trunk_layer",
    )(x, mod, *params)


def _mod_kernel(c_ref, w_ref, b_ref, o_ref):
    cond = _silu(c_ref[...])
    o_ref[0] = jnp.dot(cond.astype(BF16), w_ref[0], preferred_element_type=F32) + b_ref[0]


def _mod_call(c, w_ada_bf16, b_ada):
    depth, d, d3 = w_ada_bf16.shape
    bsz = c.shape[0]
    return pl.pallas_call(
        _mod_kernel,
        grid=(depth,),
        in_specs=[pl.BlockSpec((bsz, d), lambda l: (0, 0)),
                  pl.BlockSpec((1, d, d3), lambda l: (l, 0, 0)),
                  pl.BlockSpec((1, 1, d3), lambda l: (l, 0, 0))],
        out_specs=pl.BlockSpec((1, bsz, d3), lambda l: (l, 0, 0)),
        out_shape=jax.ShapeDtypeStruct((depth, bsz, d3), F32),
        compiler_params=pltpu.CompilerParams(dimension_semantics=("arbitrary",),
                                             vmem_limit_bytes=VMEM_LIMIT_BYTES),
        name="adaln_mod",
    )(c, w_ada_bf16, b_ada.reshape(depth, 1, d3))


def _lower_bound_kernel(logit_ref, o_ref):
    z = logit_ref[...]
    e = jnp.exp(z - jnp.max(z, axis=0, keepdims=True))
    p = e / jnp.sum(e, axis=0, keepdims=True)
    run = p[0:1, :]
    o_ref[0:1, :] = run - p[0:1, :]
    for l in range(1, z.shape[0]):
        run = run + p[l:l + 1, :]
        o_ref[l:l + 1, :] = run - p[0:1, :]


def _lower_bound_call(logits):
    return pl.pallas_call(
        _lower_bound_kernel,
        out_shape=jax.ShapeDtypeStruct(logits.shape, F32),
        name="hgrn_lower_bounds",
    )(logits.astype(F32))


def _block_diag(blocks):
    n, k, _ = blocks.shape
    eye = jnp.eye(n, dtype=blocks.dtype)
    return (eye[:, None, :, None] * blocks[:, :, None, :]).reshape(n * k, n * k)


def kernel(x, c, norm_w, w_ada, b_ada, w_in, lru_conv_w, lru_conv_b, lru_wa, lru_ba, lru_wx, lru_bx, lru_lambda, hg_lb_logits, hg_norm_w, ssd_conv_w, ssd_conv_b, ssd_dt_bias, ssd_a_log, ssd_d, ssd_norm_w, w_out, final_norm_w):
    depth = w_in.shape[0]
    d = x.shape[-1]
    bsz = x.shape[0]
    mod_all = _mod_call(c, w_ada.astype(BF16), b_ada).reshape(depth, bsz, 3, d)
    lower = _lower_bound_call(hg_lb_logits)
    hpb = LRU_HEADS // 2

    def row(v):
        return v.reshape(1, -1).astype(F32)

    def pad_row(v):
        return jnp.pad(v.astype(F32), (0, DT_PAD - v.shape[0])).reshape(1, DT_PAD)

    for l in range(depth):
        w_all = jnp.pad(w_in[l], ((0, 0), (0, N_ALL - w_in.shape[-1]))).astype(BF16)
        w_gate = jnp.stack([
            jnp.concatenate([_block_diag(lru_wa[l, hf * hpb:(hf + 1) * hpb]),
                             _block_diag(lru_wx[l, hf * hpb:(hf + 1) * hpb])], axis=1)
            for hf in range(2)]).astype(BF16)
        params = (
            row(norm_w[l]), w_all,
            lru_conv_w[l].astype(F32), row(lru_conv_b[l]), w_gate,
            row(lru_ba[l]), row(lru_bx[l]), row(lru_lambda[l]),
            row(lower[l]), row(hg_norm_w[l]),
            ssd_conv_w[l].astype(F32), row(ssd_conv_b[l]),
            pad_row(ssd_dt_bias[l]), pad_row(ssd_a_log[l]),
            row(jnp.repeat(ssd_d[l], SSD_P)), row(ssd_norm_w[l]),
            w_out[l].astype(BF16), row(final_norm_w),
        )
        x = _layer_call(x, mod_all[l], params, final=(l == depth - 1))
    return x
```

```python
import functools
import math

import jax
import jax.numpy as jnp
from jax import lax
from jax.experimental import pallas as pl
from jax.experimental.pallas import tpu as pltpu

F32 = jnp.float32
BF16 = jnp.bfloat16

EPS = 1e-6
LOG2E = math.log2(math.e)
CONV_K = 4
LRU_W = 512
LRU_HEADS = 8
LRU_C = 8.0
HG_W = 512
HG_HD = 128
HG_HEADS = 4
HG_CHUNK = 64
SSD_W = 1024
SSD_P = 64
SSD_HEADS = 16
SSD_GROUPS = 2
SSD_N = 128
SSD_CHUNK = 128
SSD_XBC = SSD_W + 2 * SSD_GROUPS * SSD_N
LANES = 128
SUBLANES = 8
DT_PAD = LANES
OFF_A = 0
OFF_B = OFF_A + 2 * LRU_W
OFF_C = OFF_B + 4 * HG_W
N_MAIN = OFF_C + SSD_W + SSD_XBC
N_ALL = N_MAIN + DT_PAD
TILE_ROWS = 256
VMEM_LIMIT_BYTES = 56 * 1024 * 1024


def _half_silu(t):
    return t * (jnp.tanh(t) + 1.0)


def _silu(x):
    return _half_silu(0.5 * x)


def _softplus(x):
    return jnp.maximum(x, 0.0) + jnp.log1p(jnp.exp(-jnp.abs(x)))


def _rms(x):
    return x * lax.rsqrt(jnp.mean(x * x, axis=-1, keepdims=True) + EPS)


def _dot(a, b):
    return jnp.dot(a.astype(BF16), b.astype(BF16), preferred_element_type=F32)


def _dot_nt(a, b):
    return lax.dot_general(a.astype(BF16), b.astype(BF16), (((1,), (1,)), ((), ())),
                           preferred_element_type=F32)


def _dot_tn(a, b):
    return lax.dot_general(a.astype(BF16), b.astype(BF16), (((0,), (0,)), ((), ())),
                           preferred_element_type=F32)


def _split(x):
    hi = x.astype(BF16)
    lo = (x - hi.astype(F32)).astype(BF16)
    return hi, lo


def _dot_exact_rhs(t, x):
    hi, lo = _split(x)
    return (jnp.dot(t, hi, preferred_element_type=F32)
            + jnp.dot(t, lo, preferred_element_type=F32))


def _dot_exact_lhs(x, t):
    hi, lo = _split(x)
    return (jnp.dot(hi, t, preferred_element_type=F32)
            + jnp.dot(lo, t, preferred_element_type=F32))


def _causal_conv(pad_ref, w_ref, b_ref, rows):
    xp = pad_ref[...]
    out = b_ref[...]
    for k in range(CONV_K):
        back = CONV_K - 1 - k
        sh = pltpu.roll(xp, back, 0) if back else xp
        out = out + sh[SUBLANES:SUBLANES + rows, :] * w_ref[k:k + 1, :]
    return out


def _lru_scan(a, b, a_scr, b_scr, carry_ref, rows):
    width = a.shape[-1]
    rid = lax.broadcasted_iota(jnp.int32, (rows, width), 0) & (SUBLANES - 1)
    for s in (1, 2, 4):
        keep = rid >= s
        a_sh = jnp.where(keep, pltpu.roll(a, s, 0), 1.0)
        b_sh = jnp.where(keep, pltpu.roll(b, s, 0), 0.0)
        b = a * b_sh + b
        a = a * a_sh
    a_scr[...] = a
    b_scr[...] = b

    def body(g, hc):
        r0 = pl.multiple_of(g * SUBLANES, SUBLANES)
        hg = b_scr[pl.ds(r0, SUBLANES), :] + a_scr[pl.ds(r0, SUBLANES), :] * hc
        b_scr[pl.ds(r0, SUBLANES), :] = hg
        return jnp.broadcast_to(hg[SUBLANES - 1:SUBLANES, :], (SUBLANES, width))

    carry_ref[...] = lax.fori_loop(0, rows // SUBLANES, body, carry_ref[...])
    return b_scr[...]


def _layer_kernel(x_ref, mod_ref, normw_ref, wall_ref,
                  lcw_ref, lcb_ref, wg_ref, lba_ref, lbx_ref, lam_ref,
                  lb_ref, hgnw_ref,
                  scw_ref, scb_ref, dtb_ref, alog_ref, dexp_ref, snw_ref,
                  wout_ref, fnw_ref, trilh_ref, trils_ref, expand_ref,
                  xo_ref,
                  conv_a, conv_c, lru_a, lru_b, lru_h, hg_st, ssd_st, o_scr, ys_scr, y_scr,
                  *, rows, final):
    j = pl.program_id(1)

    @pl.when(j == 0)
    def _reset_state():
        conv_a[0:SUBLANES, :] = jnp.zeros((SUBLANES, LRU_W), F32)
        conv_c[0:SUBLANES, :] = jnp.zeros((SUBLANES, SSD_XBC), F32)
        lru_h[...] = jnp.zeros_like(lru_h)
        hg_st[...] = jnp.zeros_like(hg_st)
        ssd_st[...] = jnp.zeros_like(ssd_st)

    x = x_ref[0]
    shift = mod_ref[0, 0:1, :]
    scale = mod_ref[0, 1:2, :]
    gate = mod_ref[0, 2:3, :]
    h = _rms(x) * normw_ref[...] * (1.0 + scale) + shift
    hb = h.astype(BF16)

    u_a = jnp.dot(hb, wall_ref[:, OFF_A:OFF_B], preferred_element_type=F32)
    conv_a[SUBLANES:SUBLANES + rows, :] = u_a[:, :LRU_W]
    xa2 = _causal_conv(conv_a, lcw_ref, lcb_ref, rows)
    conv_a[0:SUBLANES, :] = conv_a[rows:rows + SUBLANES, :]
    xab = xa2.astype(BF16)
    half = LRU_W // 2
    pre0 = jnp.dot(xab[:, :half], wg_ref[0], preferred_element_type=F32)
    pre1 = jnp.dot(xab[:, half:], wg_ref[1], preferred_element_type=F32)
    t_r = jnp.tanh(jnp.concatenate([pre0[:, :half], pre1[:, :half]], axis=-1) + lba_ref[...])
    t_i = jnp.tanh(jnp.concatenate([pre0[:, half:], pre1[:, half:]], axis=-1) + lbx_ref[...])
    rate = (-0.5 * LRU_C * LOG2E) * _softplus(-lam_ref[...])
    a = jnp.exp2(rate * (t_r + 1.0))
    s = 1.0 - a * a
    mult = jnp.where(s > 0.0, s * lax.rsqrt(s), 0.0)
    hs = _lru_scan(a, mult * ((t_i + 1.0) * xa2), lru_a, lru_b, lru_h, rows)
    y_scr[:, 0:LRU_W] = (hs * _half_silu(u_a[:, LRU_W:])).astype(BF16)

    u_b = jnp.dot(hb, wall_ref[:, OFF_B:OFF_C], preferred_element_type=F32)
    lb = lb_ref[...]
    c1 = 0.5 * (1.0 - lb)
    tt = c1 * jnp.tanh(u_b[:, HG_W:2 * HG_W])
    g_log2 = jnp.log((lb + c1) + tt) * LOG2E
    k_in = c1 - tt
    q_in = _half_silu(u_b[:, 0:HG_W])
    v_in = u_b[:, 2 * HG_W:3 * HG_W]
    cum = _dot_exact_rhs(trilh_ref[...], g_log2)
    ct = lax.broadcasted_iota(jnp.int32, (HG_CHUNK, HG_CHUNK), 0)
    cs = lax.broadcasted_iota(jnp.int32, (HG_CHUNK, HG_CHUNK), 1)
    causal_h = cs <= ct
    mid = HG_CHUNK // 2 - 1
    for c in range(rows // HG_CHUNK):
        r0 = c * HG_CHUNK
        cum_c = cum[r0:r0 + HG_CHUNK, :]
        ref_row = cum_c[mid:mid + 1, :]
        last = cum_c[HG_CHUNK - 1:HG_CHUNK, :]
        q_hat = q_in[r0:r0 + HG_CHUNK, :] * jnp.exp2(cum_c - ref_row)
        k_hat = k_in[r0:r0 + HG_CHUNK, :] * jnp.exp2(ref_row - cum_c)
        q_st = q_hat * jnp.exp2(ref_row)
        k_st = k_hat * jnp.exp2(last - ref_row)
        dec = jnp.exp2(last)
        v_c = v_in[r0:r0 + HG_CHUNK, :]
        for hd in range(HG_HEADS):
            ls = slice(hd * HG_HD, (hd + 1) * HG_HD)
            sc = jnp.where(causal_h, _dot_nt(q_hat[:, ls], k_hat[:, ls]), 0.0)
            st = hg_st[hd]
            o_scr[r0:r0 + HG_CHUNK, ls] = _dot(sc, v_c[:, ls]) + _dot_nt(q_st[:, ls], st)
            hg_st[hd] = st * dec[:, ls] + _dot_tn(v_c[:, ls], k_st[:, ls])
    yb = _rms(o_scr[...]) * hgnw_ref[...] * _half_silu(u_b[:, 3 * HG_W:4 * HG_W])
    y_scr[:, LRU_W:LRU_W + HG_W] = yb.astype(BF16)

    u_c = jnp.dot(hb, wall_ref[:, OFF_C:N_ALL], preferred_element_type=F32)
    conv_c[SUBLANES:SUBLANES + rows, :] = u_c[:, SSD_W:SSD_W + SSD_XBC]
    xbc = _half_silu(_causal_conv(conv_c, scw_ref, scb_ref, rows))
    conv_c[0:SUBLANES, :] = conv_c[rows:rows + SUBLANES, :]
    xs = xbc[:, 0:SSD_W]
    b_m = xbc[:, SSD_W:SSD_W + SSD_GROUPS * SSD_N]
    c_m = xbc[:, SSD_W + SSD_GROUPS * SSD_N:]
    dt = _softplus(u_c[:, SSD_W + SSD_XBC:] + dtb_ref[...])
    d_a = dt * (-LOG2E * jnp.exp(alog_ref[...]))
    cum_s = _dot_exact_rhs(trils_ref[...], d_a)
    expand = expand_ref[...]
    dt_e = _dot_exact_lhs(dt, expand)
    cum_e = _dot_exact_lhs(cum_s, expand)
    xdt = xs * dt_e
    ecum = jnp.exp2(cum_e)
    lane = lax.broadcasted_iota(jnp.int32, (SSD_CHUNK, LANES), 1)
    low_half = lane < SSD_P
    st_ = lax.broadcasted_iota(jnp.int32, (SSD_CHUNK, SSD_CHUNK), 0)
    ss_ = lax.broadcasted_iota(jnp.int32, (SSD_CHUNK, SSD_CHUNK), 1)
    causal_s = ss_ <= st_
    gw = SSD_W // SSD_GROUPS
    for c in range(rows // SSD_CHUNK):
        r0 = c * SSD_CHUNK
        rs = slice(r0, r0 + SSD_CHUNK)
        cum_c = cum_s[rs, :]
        cum_t = cum_c.T
        last_e = cum_e[r0 + SSD_CHUNK - 1:r0 + SSD_CHUNK, :]
        xw = xdt[rs, :] * jnp.exp2(last_e - cum_e[rs, :])
        dec_e = jnp.exp2(last_e)
        for g in range(SSD_GROUPS):
            gl = slice(g * gw, (g + 1) * gw)
            b_g = b_m[rs, g * SSD_N:(g + 1) * SSD_N]
            c_g = c_m[rs, g * SSD_N:(g + 1) * SSD_N]
            cb = _dot_nt(c_g, b_g)
            st = ssd_st[:, gl]
            y_inter = _dot(c_g, st) * ecum[rs, gl]
            for p in range(gw // LANES):
                h0 = g * (SSD_HEADS // SSD_GROUPS) + 2 * p
                l0 = g * gw + p * LANES
                mats = []
                for hh in (h0, h0 + 1):
                    diff = cum_c[:, hh:hh + 1] - cum_t[hh:hh + 1, :]
                    mats.append(jnp.where(causal_s, cb * jnp.exp2(diff), 0.0).astype(BF16))
                xb = xdt[rs, l0:l0 + LANES]
                x2 = jnp.concatenate([jnp.where(low_half, xb, 0.0),
                                      jnp.where(low_half, 0.0, xb)], axis=0).astype(BF16)
                y_pair = jnp.dot(jnp.concatenate(mats, axis=-1), x2, preferred_element_type=F32)
                ys_scr[rs, l0:l0 + LANES] = y_pair + y_inter[:, p * LANES:(p + 1) * LANES]
            ssd_st[:, gl] = st * dec_e[:, gl] + _dot_tn(b_g, xw[:, gl])
    yc = (ys_scr[...] + dexp_ref[...] * xs) * _half_silu(u_c[:, 0:SSD_W])
    for g in range(SSD_GROUPS):
        gl = slice(g * gw, (g + 1) * gw)
        y_scr[:, LRU_W + HG_W + g * gw:LRU_W + HG_W + (g + 1) * gw] = (
            _rms(yc[:, gl]) * snw_ref[:, gl]).astype(BF16)

    xn = x + gate * jnp.dot(y_scr[...], wout_ref[...], preferred_element_type=F32)
    if final:
        xn = _rms(xn) * fnw_ref[...]
    xo_ref[0] = xn


def _layer_spec(arr, layer):
    nd = arr.ndim - 1
    return pl.BlockSpec((None,) + arr.shape[1:], lambda b, j: (layer,) + (0,) * nd,
                        pipeline_mode=pl.Buffered(1))


def _const_spec(arr):
    nd = arr.ndim
    return pl.BlockSpec(arr.shape, lambda b, j: (0,) * nd, pipeline_mode=pl.Buffered(1))


def _layer_call(x, mod, stacked, consts, layer, *, final):
    bsz, seq, d = x.shape
    rows = min(TILE_ROWS, seq)
    assert seq % rows == 0 and rows % SSD_CHUNK == 0 and rows % HG_CHUNK == 0
    in_specs = ([pl.BlockSpec((1, rows, d), lambda b, j: (b, j, 0)),
                 pl.BlockSpec((None, 1, 3, d), lambda b, j: (layer, b, 0, 0))]
                + [_layer_spec(p, layer) for p in stacked]
                + [_const_spec(p) for p in consts])
    scratch = [
        pltpu.VMEM((rows + SUBLANES, LRU_W), F32),
        pltpu.VMEM((rows + SUBLANES, SSD_XBC), F32),
        pltpu.VMEM((rows, LRU_W), F32),
        pltpu.VMEM((rows, LRU_W), F32),
        pltpu.VMEM((SUBLANES, LRU_W), F32),
        pltpu.VMEM((HG_HEADS, HG_HD, HG_HD), F32),
        pltpu.VMEM((SSD_N, SSD_W), F32),
        pltpu.VMEM((rows, HG_W), F32),
        pltpu.VMEM((rows, SSD_W), F32),
        pltpu.VMEM((rows, 2 * d), BF16),
    ]
    return pl.pallas_call(
        functools.partial(_layer_kernel, rows=rows, final=final),
        grid=(bsz, seq // rows),
        in_specs=in_specs,
        out_specs=pl.BlockSpec((1, rows, d), lambda b, j: (b, j, 0)),
        out_shape=jax.ShapeDtypeStruct(x.shape, x.dtype),
        scratch_shapes=scratch,
        compiler_params=pltpu.CompilerParams(
            dimension_semantics=("arbitrary", "arbitrary"),
            vmem_limit_bytes=VMEM_LIMIT_BYTES),
        name="trunk_layer",
    )(x, mod, *stacked, *consts)


def _mod_kernel(c_ref, w_ref, b_ref, o_ref):
    cond = _silu(c_ref[...])
    o_ref[0] = jnp.dot(cond.astype(BF16), w_ref[0], preferred_element_type=F32) + b_ref[0]


def _mod_call(c, w_ada_bf16, b_ada):
    depth, d, d3 = w_ada_bf16.shape
    bsz = c.shape[0]
    return pl.pallas_call(
        _mod_kernel,
        grid=(depth,),
        in_specs=[pl.BlockSpec((bsz, d), lambda l: (0, 0)),
                  pl.BlockSpec((1, d, d3), lambda l: (l, 0, 0)),
                  pl.BlockSpec((1, 1, d3), lambda l: (l, 0, 0))],
        out_specs=pl.BlockSpec((1, bsz, d3), lambda l: (l, 0, 0)),
        out_shape=jax.ShapeDtypeStruct((depth, bsz, d3), F32),
        compiler_params=pltpu.CompilerParams(dimension_semantics=("arbitrary",),
                                             vmem_limit_bytes=VMEM_LIMIT_BYTES),
        name="adaln_mod",
    )(c, w_ada_bf16, b_ada.reshape(depth, 1, d3))


def _lower_bound_kernel(logit_ref, o_ref):
    z = logit_ref[...]
    e = jnp.exp(z - jnp.max(z, axis=0, keepdims=True))
    p = e / jnp.sum(e, axis=0, keepdims=True)
    run = p[0:1, :]
    o_ref[0:1, :] = run - p[0:1, :]
    for l in range(1, z.shape[0]):
        run = run + p[l:l + 1, :]
        o_ref[l:l + 1, :] = run - p[0:1, :]


def _lower_bound_call(logits):
    return pl.pallas_call(
        _lower_bound_kernel,
        out_shape=jax.ShapeDtypeStruct(logits.shape, F32),
        name="hgrn_lower_bounds",
    )(logits.astype(F32))


def _block_diag(blocks):
    depth, n, k, _ = blocks.shape
    eye = jnp.eye(n, dtype=blocks.dtype)
    return (eye[None, :, None, :, None] * blocks[:, :, :, None, :]).reshape(depth, n * k, n * k)


def _chunk_tril(rows, chunk):
    t = jnp.arange(rows)[:, None]
    s = jnp.arange(rows)[None, :]
    return ((s <= t) & (t // chunk == s // chunk)).astype(BF16)


def kernel(x, c, norm_w, w_ada, b_ada, w_in, lru_conv_w, lru_conv_b, lru_wa, lru_ba, lru_wx, lru_bx, lru_lambda, hg_lb_logits, hg_norm_w, ssd_conv_w, ssd_conv_b, ssd_dt_bias, ssd_a_log, ssd_d, ssd_norm_w, w_out, final_norm_w):
    depth = w_in.shape[0]
    bsz, seq, d = x.shape
    rows = min(TILE_ROWS, seq)
    mod_all = _mod_call(c, w_ada.astype(BF16), b_ada).reshape(depth, bsz, 3, d)
    lower = _lower_bound_call(hg_lb_logits)

    def rowvec(v):
        return v.astype(F32)[:, None, :]

    def pad_rowvec(v):
        return jnp.pad(v.astype(F32), ((0, 0), (0, DT_PAD - v.shape[-1])))[:, None, :]

    col_scale = jnp.concatenate([
        jnp.full((LRU_W,), 1.0), jnp.full((LRU_W,), 0.5),
        jnp.full((2 * HG_W,), 0.5), jnp.full((HG_W,), 1.0), jnp.full((HG_W,), 0.5),
        jnp.full((SSD_W,), 0.5), jnp.full((SSD_XBC + SSD_HEADS,), 1.0)]).astype(F32)
    w_all = jnp.pad(w_in * col_scale, ((0, 0), (0, 0), (0, N_ALL - w_in.shape[-1]))).astype(BF16)
    hpb = LRU_HEADS // 2
    w_gate = jnp.stack([
        jnp.concatenate([_block_diag(lru_wa[:, hf * hpb:(hf + 1) * hpb]),
                         _block_diag(lru_wx[:, hf * hpb:(hf + 1) * hpb])], axis=-1)
        for hf in range(2)], axis=1).astype(BF16)
    stacked = (
        rowvec(norm_w), w_all,
        0.5 * lru_conv_w.astype(F32), rowvec(0.5 * lru_conv_b), w_gate,
        rowvec(0.5 * lru_ba.reshape(depth, -1)), rowvec(0.5 * lru_bx.reshape(depth, -1)),
        rowvec(lru_lambda),
        rowvec(lower), rowvec(hg_norm_w),
        0.5 * ssd_conv_w.astype(F32), rowvec(0.5 * ssd_conv_b),
        pad_rowvec(ssd_dt_bias), pad_rowvec(ssd_a_log),
        rowvec(jnp.repeat(ssd_d, SSD_P, axis=-1)), rowvec(ssd_norm_w),
        w_out.astype(BF16),
    )
    head_of_lane = jnp.arange(SSD_W) // SSD_P
    consts = (
        final_norm_w.astype(F32).reshape(1, d),
        _chunk_tril(rows, HG_CHUNK), _chunk_tril(rows, SSD_CHUNK),
        (jnp.arange(DT_PAD)[:, None] == head_of_lane[None, :]).astype(BF16),
    )
    for l in range(depth):
        x = _layer_call(x, mod_all, stacked, consts, l, final=(l == depth - 1))
    return x
```

```python
import functools
import math

import jax
import jax.numpy as jnp
from jax import lax
from jax.experimental import pallas as pl
from jax.experimental.pallas import tpu as pltpu

F32 = jnp.float32
BF16 = jnp.bfloat16

EPS = 1e-6
LOG2E = math.log2(math.e)
CONV_K = 4
LRU_W = 512
LRU_HEADS = 8
LRU_C = 8.0
HG_W = 512
HG_HD = 128
HG_HEADS = 4
HG_CHUNK = 64
SSD_W = 1024
SSD_P = 64
SSD_HEADS = 16
SSD_GROUPS = 2
SSD_N = 128
SSD_CHUNK = 128
SSD_XBC = SSD_W + 2 * SSD_GROUPS * SSD_N
LANES = 128
SUBLANES = 8
DT_PAD = LANES
OFF_A = 0
OFF_B = OFF_A + 2 * LRU_W
OFF_C = OFF_B + 4 * HG_W
N_MAIN = OFF_C + SSD_W + SSD_XBC
N_ALL = N_MAIN + DT_PAD
TILE_ROWS = 256
PROJ_CHUNK = 256
REPACK_COLS = 1408
assert N_MAIN % PROJ_CHUNK == 0
N_CHUNKS = N_MAIN // PROJ_CHUNK + 1
HG_BLOCK = 16
HG_SAFE_SPAN = 120.0
PRELUDE_CHUNKS = 2
LRU_SLOTS = 3
REST_STAGES_PER_CHUNK = 1
VMEM_LIMIT_BYTES = 56 * 1024 * 1024
SCHED_RANGE = 4000


def _half_silu(t):
    return t * (jnp.tanh(t) + 1.0)


def _silu(x):
    return _half_silu(0.5 * x)


def _softplus(x):
    return jnp.maximum(x, 0.0) + jnp.log1p(jnp.exp(-jnp.abs(x)))


def _rms(x):
    return x * lax.rsqrt(jnp.mean(x * x, axis=-1, keepdims=True) + EPS)


def _dot(a, b):
    return jnp.dot(a.astype(BF16), b.astype(BF16), preferred_element_type=F32)


def _dot_nt(a, b):
    return lax.dot_general(a.astype(BF16), b.astype(BF16), (((1,), (1,)), ((), ())),
                           preferred_element_type=F32)


def _dot_tn(a, b):
    return lax.dot_general(a.astype(BF16), b.astype(BF16), (((0,), (0,)), ((), ())),
                           preferred_element_type=F32)


def _split(x):
    hi = x.astype(BF16)
    lo = (x - hi.astype(F32)).astype(BF16)
    return hi, lo


def _dot_exact_rhs(t, x):
    hi, lo = _split(x)
    return (jnp.dot(t, hi, preferred_element_type=F32)
            + jnp.dot(t, lo, preferred_element_type=F32))


def _expand_heads(x, expand2_ref):
    hi, lo = _split(x)
    return jnp.dot(jnp.concatenate([hi, lo], axis=-1), expand2_ref[...],
                   preferred_element_type=F32)


def _causal_conv(pad_ref, w_ref, b_ref, rows):
    xp = pad_ref[...]
    out = b_ref[...]
    for k in range(CONV_K):
        back = CONV_K - 1 - k
        sh = pltpu.roll(xp, back, 0) if back else xp
        out = out + sh[SUBLANES:SUBLANES + rows, :] * w_ref[k:k + 1, :]
    return out


def _lru_scan(a, b, carry_ref, rows):
    width = a.shape[-1]
    groups_shape = (rows // SUBLANES, SUBLANES, width)
    a = a.reshape(groups_shape)
    b = b.reshape(groups_shape)
    rid = lax.broadcasted_iota(jnp.int32, groups_shape, 1)
    for s in (1, 2, 4):
        keep = rid >= s
        a_sh = jnp.where(keep, pltpu.roll(a, s, 1), 1.0)
        b_sh = jnp.where(keep, pltpu.roll(b, s, 1), 0.0)
        b = a * b_sh + b
        a = a * a_sh
    a = a.reshape(rows, width)
    b = b.reshape(rows, width)
    hc = carry_ref[...]
    groups = []
    for g in range(rows // SUBLANES):
        gs = slice(g * SUBLANES, (g + 1) * SUBLANES)
        hg = b[gs, :] + a[gs, :] * hc
        groups.append(hg)
        hc = jnp.broadcast_to(hg[SUBLANES - 1:SUBLANES, :], (SUBLANES, width))
    carry_ref[...] = hc
    return jnp.concatenate(groups, axis=0)


def _round_robin(*gens):
    live = list(gens)
    while live:
        for gen in list(live):
            try:
                next(gen)
            except StopIteration:
                live.remove(gen)
            else:
                yield


def _layer_kernel(x_ref, mod_ref, normw_ref, wall_ref, wdt_ref,
                  lcw_ref, lcb_ref, wg_ref, lba_ref, lbx_ref, lam_ref,
                  lb_ref, hgnw_ref,
                  scw_ref, scb_ref, dtb_ref, alog_ref, dexp_ref, snw_ref,
                  wout_ref, fnw_ref, trilh_ref, trils_ref, expand_ref,
                  xo_ref, risk_ref,
                  u_even, u_odd, x_even, x_odd, gate_even, gate_odd,
                  conv_a, conv_c, lru_h, hg_st, ssd_st, o_scr, ys_scr, y_scr, hg_tmp,
                  *, rows, final, n_time, n_tiles, exact):
    g = pl.program_id(0)
    tile_b = jnp.maximum(g - 1, 0)

    @pl.when(g == 0)
    def _clear_unprojected():
        u_odd[...] = jnp.zeros_like(u_odd)
        x_odd[...] = jnp.zeros_like(x_odd)
        gate_odd[...] = jnp.zeros_like(gate_odd)
        risk_ref[0] = jnp.int32(0)

    @pl.when(tile_b % n_time == 0)
    def _reset_state():
        conv_a[0:SUBLANES, :] = jnp.zeros((SUBLANES, LRU_W), F32)
        conv_c[0:SUBLANES, :] = jnp.zeros((SUBLANES, SSD_XBC), F32)
        lru_h[...] = jnp.zeros_like(lru_h)
        hg_st[...] = jnp.zeros_like(hg_st)
        ssd_st[...] = jnp.zeros_like(ssd_st)

    refs = (mod_ref, normw_ref, wall_ref, lcw_ref, lcb_ref, wg_ref, lba_ref, lbx_ref, lam_ref,
            lb_ref, hgnw_ref, scw_ref, scb_ref, dtb_ref, alog_ref, dexp_ref, snw_ref,
            wout_ref, fnw_ref, trilh_ref, trils_ref, expand_ref)
    state = (conv_a, conv_c, lru_h, hg_st, ssd_st, o_scr, ys_scr, y_scr, hg_tmp)
    batch_a = jnp.minimum(g, n_tiles - 1) // n_time
    batch_b = tile_b // n_time

    def step(par_w, par_r):
        u_w, x_w, gate_w = u_buf[par_w], x_buf[par_w], gate_buf[par_w]
        u_r, x_r, gate_r = u_buf[par_r], x_buf[par_r], gate_buf[par_r]
        hb = _norm_in(x_ref, mod_ref, normw_ref, batch_a, x_w)
        done = 0

        def project(n_more):
            nonlocal done
            for i in range(done, min(done + n_more, N_CHUNKS)):
                _proj_chunk(hb, wall_ref, wdt_ref, u_w, i)
            done = min(done + n_more, N_CHUNKS)

        project(N_CHUNKS if exact else PRELUDE_CHUNKS)
        for slots in _lru_stages(refs, state, u_r, rows=rows):
            project(slots)
        assert done * PROJ_CHUNK >= OFF_B + 2 * HG_W
        span = _hgrn_gates(lb_ref, trilh_ref, u_w, gate_w, rows)
        unsafe = jnp.where(span > HG_SAFE_SPAN, 1, 0).astype(jnp.int32)
        risk_ref[0] = jnp.maximum(risk_ref[0], unsafe)
        hg_in = _hgrn_inputs(lb_ref, u_r, gate_r)
        for slots in _mix_rest(refs, state, batch_b, u_r, x_r, xo_ref, hg_in,
                               rows=rows, final=final, exact=exact):
            project(slots)
        project(N_CHUNKS)

    u_buf, x_buf, gate_buf = (u_even, u_odd), (x_even, x_odd), (gate_even, gate_odd)
    for par_w in (0, 1):
        pl.when(g % 2 == par_w)(functools.partial(step, par_w, 1 - par_w))


def _norm_in(x_ref, mod_ref, normw_ref, batch, x_w):
    x = x_ref[0]
    shift = mod_ref[batch, 0:1, :]
    scale = mod_ref[batch, 1:2, :]
    h = _rms(x) * normw_ref[...] * (1.0 + scale) + shift
    x_w[...] = x
    return h.astype(BF16)


def _proj_chunk(hb, wall_ref, wdt_ref, u_w, i):
    if i < N_CHUNKS - 1:
        lo, hi = i * PROJ_CHUNK, (i + 1) * PROJ_CHUNK
        u_w[:, lo:hi] = jnp.dot(hb, wall_ref[:, lo:hi], preferred_element_type=F32)
    else:
        u_w[:, N_MAIN:N_ALL] = jnp.dot(hb, wdt_ref[...], preferred_element_type=F32)


def _hgrn_gates(lb_ref, trilh_ref, u_ref, gate_w, rows):
    lb = lb_ref[...]
    c1 = 0.5 * (1.0 - lb)
    tt = c1 * jnp.tanh(u_ref[:, OFF_B + HG_W:OFF_B + 2 * HG_W])
    g_log2 = jnp.log((lb + c1) + tt) * LOG2E
    cum = _dot_exact_rhs(trilh_ref[...], g_log2)
    gate_w[:, 0:HG_W] = tt
    gate_w[:, HG_W:2 * HG_W] = cum
    mid = HG_CHUNK // 2 - 1
    spans = []
    for r in range(0, rows, HG_CHUNK):
        to_mid = cum[r + mid:r + mid + 1, :]
        spans += [-to_mid, to_mid - cum[r + HG_CHUNK - 1:r + HG_CHUNK, :]]
    return jnp.max(jnp.concatenate(spans, axis=0))


def _hgrn_inputs(lb_ref, u_ref, gate_r):
    u_b = u_ref.at[:, OFF_B:OFF_C]
    c1 = 0.5 * (1.0 - lb_ref[...])
    k_in = c1 - gate_r[:, 0:HG_W]
    return (_half_silu(u_b[:, 0:HG_W]), k_in, u_b[:, 2 * HG_W:3 * HG_W],
            gate_r[:, HG_W:2 * HG_W])


def _hgrn_exact(hg_in, tmp_ref, hg_st, o_scr, rows):
    q_in, k_in, v_in, cum = hg_in
    tmp_ref[:, 0:HG_W] = q_in
    tmp_ref[:, HG_W:2 * HG_W] = k_in
    tmp_ref[:, 2 * HG_W:3 * HG_W] = v_in
    tmp_ref[:, 3 * HG_W:4 * HG_W] = cum
    row = lax.broadcasted_iota(jnp.int32, (HG_BLOCK, HG_HD), 0)
    blocks_per_chunk = HG_CHUNK // HG_BLOCK

    def block(i, carry):
        r0 = pl.multiple_of(i * HG_BLOCK, HG_BLOCK)
        rs = pl.ds(r0, HG_BLOCK)
        before = tmp_ref[pl.ds(jnp.maximum(r0 - 1, 0), 1), 3 * HG_W:4 * HG_W]
        base = jnp.where(i % blocks_per_chunk == 0, 0.0, before)
        cb = tmp_ref[rs, 3 * HG_W:4 * HG_W] - base
        last = cb[HG_BLOCK - 1:HG_BLOCK, :]
        q_b = tmp_ref[rs, 0:HG_W]
        k_b = tmp_ref[rs, HG_W:2 * HG_W]
        v_b = tmp_ref[rs, 2 * HG_W:3 * HG_W]
        q_st = q_b * jnp.exp2(cb)
        k_st = k_b * jnp.exp2(last - cb)
        dec = jnp.exp2(last)
        for hd in range(HG_HEADS):
            ls = slice(hd * HG_HD, (hd + 1) * HG_HD)
            st = hg_st[hd]
            acc = _dot_nt(q_st[:, ls], st)
            for s in range(HG_BLOCK):
                seen = row >= s
                decay = jnp.exp2(jnp.where(seen, cb[:, ls] - cb[s:s + 1, ls], 0.0))
                score = jnp.sum(jnp.where(seen, q_b[:, ls] * k_b[s:s + 1, ls] * decay, 0.0),
                                axis=-1, keepdims=True)
                acc = acc + score * v_b[s:s + 1, ls]
            o_scr[rs, ls] = acc
            hg_st[hd] = st * dec[:, ls] + _dot_tn(v_b[:, ls], k_st[:, ls])
        return carry

    lax.fori_loop(0, rows // HG_BLOCK, block, 0)


def _lru_stages(refs, state, u_ref, *, rows):
    (mod_ref, normw_ref, wall_ref, lcw_ref, lcb_ref, wg_ref, lba_ref, lbx_ref, lam_ref,
     lb_ref, hgnw_ref, scw_ref, scb_ref, dtb_ref, alog_ref, dexp_ref, snw_ref,
     wout_ref, fnw_ref, trilh_ref, trils_ref, expand_ref) = refs
    conv_a, conv_c, lru_h, hg_st, ssd_st, o_scr, ys_scr, y_scr, hg_tmp = state
    u_a = u_ref.at[:, OFF_A:OFF_B]
    conv_a[SUBLANES:SUBLANES + rows, :] = u_a[:, :LRU_W]
    xa2 = _causal_conv(conv_a, lcw_ref, lcb_ref, rows)
    conv_a[0:SUBLANES, :] = conv_a[rows:rows + SUBLANES, :]
    yield LRU_SLOTS
    xab = xa2.astype(BF16)
    half = LRU_W // 2
    pre0 = jnp.dot(xab[:, :half], wg_ref[0], preferred_element_type=F32)
    pre1 = jnp.dot(xab[:, half:], wg_ref[1], preferred_element_type=F32)
    t_r = jnp.tanh(jnp.concatenate([pre0[:, :half], pre1[:, :half]], axis=-1) + lba_ref[...])
    t_i = jnp.tanh(jnp.concatenate([pre0[:, half:], pre1[:, half:]], axis=-1) + lbx_ref[...])
    yield LRU_SLOTS
    rate = (-0.5 * LRU_C * LOG2E) * _softplus(-lam_ref[...])
    a = jnp.exp2(rate * (t_r + 1.0))
    s = 1.0 - a * a
    mult = jnp.where(s > 0.0, s * lax.rsqrt(s), 0.0)
    yield LRU_SLOTS
    hs = _lru_scan(a, mult * ((t_i + 1.0) * xa2), lru_h, rows)
    y_scr[:, 0:LRU_W] = (hs * _half_silu(u_a[:, LRU_W:])).astype(BF16)
    yield LRU_SLOTS


def _mix_rest(refs, state, batch, u_ref, xres_ref, xo_ref, hg_in, *, rows, final, exact):
    (mod_ref, normw_ref, wall_ref, lcw_ref, lcb_ref, wg_ref, lba_ref, lbx_ref, lam_ref,
     lb_ref, hgnw_ref, scw_ref, scb_ref, dtb_ref, alog_ref, dexp_ref, snw_ref,
     wout_ref, fnw_ref, trilh_ref, trils_ref, expand_ref) = refs
    conv_a, conv_c, lru_h, hg_st, ssd_st, o_scr, ys_scr, y_scr, hg_tmp = state

    def hgrn_stages():
        u_b = u_ref.at[:, OFF_B:OFF_C]
        q_in, k_in, v_in, cum = hg_in
        ct = lax.broadcasted_iota(jnp.int32, (HG_CHUNK, HG_CHUNK), 0)
        cs = lax.broadcasted_iota(jnp.int32, (HG_CHUNK, HG_CHUNK), 1)
        causal_h = cs <= ct
        mid = HG_CHUNK // 2 - 1
        if exact:
            _hgrn_exact(hg_in, hg_tmp, hg_st, o_scr, rows)
            yield
        for c in (() if exact else range(rows // HG_CHUNK)):
            r0 = c * HG_CHUNK
            cum_c = cum[r0:r0 + HG_CHUNK, :]
            ref_row = cum_c[mid:mid + 1, :]
            last = cum_c[HG_CHUNK - 1:HG_CHUNK, :]
            q_hat = q_in[r0:r0 + HG_CHUNK, :] * jnp.exp2(cum_c - ref_row)
            k_hat = k_in[r0:r0 + HG_CHUNK, :] * jnp.exp2(ref_row - cum_c)
            q_st = q_hat * jnp.exp2(ref_row)
            k_st = k_hat * jnp.exp2(last - ref_row)
            dec = jnp.exp2(last)
            v_c = v_in[r0:r0 + HG_CHUNK, :]
            yield
            for hd in range(HG_HEADS):
                ls = slice(hd * HG_HD, (hd + 1) * HG_HD)
                sc = jnp.where(causal_h, _dot_nt(q_hat[:, ls], k_hat[:, ls]), 0.0)
                st = hg_st[hd]
                o_scr[r0:r0 + HG_CHUNK, ls] = _dot(sc, v_c[:, ls]) + _dot_nt(q_st[:, ls], st)
                hg_st[hd] = st * dec[:, ls] + _dot_tn(v_c[:, ls], k_st[:, ls])
                yield
        yb = _rms(o_scr[...]) * hgnw_ref[...] * _half_silu(u_b[:, 3 * HG_W:4 * HG_W])
        y_scr[:, LRU_W:LRU_W + HG_W] = yb.astype(BF16)
        yield

    def ssd_stages():
        u_c = u_ref.at[:, OFF_C:N_ALL]
        conv_c[SUBLANES:SUBLANES + rows, :] = u_c[:, SSD_W:SSD_W + SSD_XBC]
        xbc = _half_silu(_causal_conv(conv_c, scw_ref, scb_ref, rows))
        conv_c[0:SUBLANES, :] = conv_c[rows:rows + SUBLANES, :]
        yield
        xs = xbc[:, 0:SSD_W]
        b_m = xbc[:, SSD_W:SSD_W + SSD_GROUPS * SSD_N]
        c_m = xbc[:, SSD_W + SSD_GROUPS * SSD_N:]
        dt = _softplus(u_c[:, SSD_W + SSD_XBC:] + dtb_ref[...])
        d_a = dt * (-LOG2E * jnp.exp(alog_ref[...]))
        cum_s = _dot_exact_rhs(trils_ref[...], d_a)
        yield
        dt_e = _expand_heads(dt, expand_ref)
        cum_e = _expand_heads(cum_s, expand_ref)
        yield
        xdt = xs * dt_e
        ecum = jnp.exp2(cum_e)
        yield
        lane = lax.broadcasted_iota(jnp.int32, (SSD_CHUNK, LANES), 1)
        low_half = lane < SSD_P
        st_ = lax.broadcasted_iota(jnp.int32, (SSD_CHUNK, SSD_CHUNK), 0)
        ss_ = lax.broadcasted_iota(jnp.int32, (SSD_CHUNK, SSD_CHUNK), 1)
        causal_s = ss_ <= st_
        gw = SSD_W // SSD_GROUPS
        for c in range(rows // SSD_CHUNK):
            r0 = c * SSD_CHUNK
            rs = slice(r0, r0 + SSD_CHUNK)
            cum_c = cum_s[rs, :]
            cum_t = cum_c.T
            last_e = cum_e[r0 + SSD_CHUNK - 1:r0 + SSD_CHUNK, :]
            xw = xdt[rs, :] * jnp.exp2(last_e - cum_e[rs, :])
            dec_e = jnp.exp2(last_e)
            yield
            for g in range(SSD_GROUPS):
                gl = slice(g * gw, (g + 1) * gw)
                b_g = b_m[rs, g * SSD_N:(g + 1) * SSD_N]
                c_g = c_m[rs, g * SSD_N:(g + 1) * SSD_N]
                cb = _dot_nt(c_g, b_g)
                st = ssd_st[:, gl]
                y_inter = _dot(c_g, st) * ecum[rs, gl]
                yield
                for p in range(gw // LANES):
                    h0 = g * (SSD_HEADS // SSD_GROUPS) + 2 * p
                    l0 = g * gw + p * LANES
                    mats = []
                    for hh in (h0, h0 + 1):
                        diff = cum_c[:, hh:hh + 1] - cum_t[hh:hh + 1, :]
                        mats.append(jnp.where(causal_s, cb * jnp.exp2(diff), 0.0).astype(BF16))
                    xb = xdt[rs, l0:l0 + LANES]
                    x2 = jnp.concatenate([jnp.where(low_half, xb, 0.0),
                                          jnp.where(low_half, 0.0, xb)], axis=0).astype(BF16)
                    y_pair = jnp.dot(jnp.concatenate(mats, axis=-1), x2, preferred_element_type=F32)
                    ys_scr[rs, l0:l0 + LANES] = y_pair + y_inter[:, p * LANES:(p + 1) * LANES]
                    yield
                ssd_st[:, gl] = st * dec_e[:, gl] + _dot_tn(b_g, xw[:, gl])
                yield
        yc = (ys_scr[...] + dexp_ref[...] * xs) * _half_silu(u_c[:, 0:SSD_W])
        yield
        for g in range(SSD_GROUPS):
            gl = slice(g * gw, (g + 1) * gw)
            y_scr[:, LRU_W + HG_W + g * gw:LRU_W + HG_W + (g + 1) * gw] = (
                _rms(yc[:, gl]) * snw_ref[:, gl]).astype(BF16)

    for n, _ in enumerate(_round_robin(hgrn_stages(), ssd_stages())):
        yield 1 if n % REST_STAGES_PER_CHUNK == 0 else 0

    gate = mod_ref[batch, 2:3, :]
    xn = xres_ref[...] + gate * jnp.dot(y_scr[...], wout_ref[...], preferred_element_type=F32)
    if final:
        xn = _rms(xn) * fnw_ref[...]
    xo_ref[0] = xn


def _layer_spec(arr, layer):
    nd = arr.ndim - 1
    return pl.BlockSpec((None,) + arr.shape[1:], lambda g: (layer,) + (0,) * nd,
                        pipeline_mode=pl.Buffered(1))


def _const_spec(arr):
    nd = arr.ndim
    return pl.BlockSpec(arr.shape, lambda g: (0,) * nd, pipeline_mode=pl.Buffered(1))


def _layer_call(x, mod, stacked, consts, layer, *, final, exact):
    bsz, seq, d = x.shape
    rows = min(TILE_ROWS, seq)
    assert seq % rows == 0 and rows % SSD_CHUNK == 0 and rows % HG_CHUNK == 0
    n_time = seq // rows
    n_tiles = bsz * n_time

    def tile_in(g):
        t = jnp.minimum(g, n_tiles - 1)
        return (t // n_time, t % n_time, 0)

    def tile_out(g):
        t = jnp.maximum(g - 1, 0)
        return (t // n_time, t % n_time, 0)

    in_specs = ([pl.BlockSpec((1, rows, d), tile_in), _layer_spec(mod, layer)]
                + [_layer_spec(p, layer) for p in stacked]
                + [_const_spec(p) for p in consts])
    scratch = [
        pltpu.VMEM((rows, N_ALL), F32),
        pltpu.VMEM((rows, N_ALL), F32),
        pltpu.VMEM((rows, d), F32),
        pltpu.VMEM((rows, d), F32),
        pltpu.VMEM((rows, 2 * HG_W), F32),
        pltpu.VMEM((rows, 2 * HG_W), F32),
        pltpu.VMEM((rows + SUBLANES, LRU_W), F32),
        pltpu.VMEM((rows + SUBLANES, SSD_XBC), F32),
        pltpu.VMEM((SUBLANES, LRU_W), F32),
        pltpu.VMEM((HG_HEADS, HG_HD, HG_HD), F32),
        pltpu.VMEM((SSD_N, SSD_W), F32),
        pltpu.VMEM((rows, HG_W), F32),
        pltpu.VMEM((rows, SSD_W), F32),
        pltpu.VMEM((rows, 2 * d), BF16),
        pltpu.VMEM((rows, 4 * HG_W) if exact else (SUBLANES, LANES), F32),
    ]
    return pl.pallas_call(
        functools.partial(_layer_kernel, rows=rows, final=final, n_time=n_time, n_tiles=n_tiles,
                          exact=exact),
        grid=(n_tiles + 1,),
        in_specs=in_specs,
        out_specs=[pl.BlockSpec((1, rows, d), tile_out),
                   pl.BlockSpec(memory_space=pltpu.SMEM)],
        out_shape=[jax.ShapeDtypeStruct(x.shape, x.dtype),
                   jax.ShapeDtypeStruct((1,), jnp.int32)],
        scratch_shapes=scratch,
        compiler_params=pltpu.CompilerParams(
            dimension_semantics=("arbitrary",),
            vmem_limit_bytes=VMEM_LIMIT_BYTES),
        name="trunk_layer",
    )(x, mod, *stacked, *consts)


def _mod_kernel(c_ref, w_ref, b_ref, o_ref):
    cond = _silu(c_ref[...])
    o_ref[0] = _dot(cond, w_ref[0]) + b_ref[0]


def _mod_call(c, w_ada, b_ada):
    depth, d, d3 = w_ada.shape
    bsz = c.shape[0]
    return pl.pallas_call(
        _mod_kernel,
        grid=(depth,),
        in_specs=[pl.BlockSpec((bsz, d), lambda l: (0, 0)),
                  pl.BlockSpec((1, d, d3), lambda l: (l, 0, 0)),
                  pl.BlockSpec((1, 1, d3), lambda l: (l, 0, 0))],
        out_specs=pl.BlockSpec((1, bsz, d3), lambda l: (l, 0, 0)),
        out_shape=jax.ShapeDtypeStruct((depth, bsz, d3), F32),
        compiler_params=pltpu.CompilerParams(dimension_semantics=("arbitrary",),
                                             vmem_limit_bytes=VMEM_LIMIT_BYTES),
        name="adaln_mod",
    )(c, w_ada, b_ada.reshape(depth, 1, d3))


def _repack_kernel(wt_ref, scale_ref, o_ref):
    o_ref[0] = (wt_ref[0].T * scale_ref[...]).astype(BF16)


def _repack_call(w_in, col_scale):
    depth, k, n = w_in.shape
    blk = REPACK_COLS
    assert N_MAIN % blk == 0 and N_MAIN <= n
    return pl.pallas_call(
        _repack_kernel,
        grid=(depth, N_MAIN // blk),
        in_specs=[pl.BlockSpec((1, blk, k), lambda l, j: (l, j, 0)),
                  pl.BlockSpec((1, blk), lambda l, j: (0, j))],
        out_specs=pl.BlockSpec((1, k, blk), lambda l, j: (l, 0, j)),
        out_shape=jax.ShapeDtypeStruct((depth, k, N_MAIN), BF16),
        compiler_params=pltpu.CompilerParams(dimension_semantics=("arbitrary", "arbitrary"),
                                             vmem_limit_bytes=VMEM_LIMIT_BYTES),
        name="repack_w_in",
    )(jnp.swapaxes(w_in, 1, 2), col_scale)


def _lower_bound_kernel(logit_ref, o_ref):
    z = logit_ref[...]
    e = jnp.exp(z - jnp.max(z, axis=0, keepdims=True))
    p = e / jnp.sum(e, axis=0, keepdims=True)
    run = p[0:1, :]
    o_ref[0:1, :] = run - p[0:1, :]
    for l in range(1, z.shape[0]):
        run = run + p[l:l + 1, :]
        o_ref[l:l + 1, :] = run - p[0:1, :]


def _lower_bound_call(logits):
    return pl.pallas_call(
        _lower_bound_kernel,
        out_shape=jax.ShapeDtypeStruct(logits.shape, F32),
        name="hgrn_lower_bounds",
    )(logits.astype(F32))


def _block_diag(blocks):
    depth, n, k, _ = blocks.shape
    eye = jnp.eye(n, dtype=blocks.dtype)
    return (eye[None, :, None, :, None] * blocks[:, :, :, None, :]).reshape(depth, n * k, n * k)


def _chunk_tril(rows, chunk):
    t = jnp.arange(rows)[:, None]
    s = jnp.arange(rows)[None, :]
    return ((s <= t) & (t // chunk == s // chunk)).astype(BF16)


def kernel(x, c, norm_w, w_ada, b_ada, w_in, lru_conv_w, lru_conv_b, lru_wa, lru_ba, lru_wx, lru_bx, lru_lambda, hg_lb_logits, hg_norm_w, ssd_conv_w, ssd_conv_b, ssd_dt_bias, ssd_a_log, ssd_d, ssd_norm_w, w_out, final_norm_w):
    depth = w_in.shape[0]
    bsz, seq, d = x.shape
    rows = min(TILE_ROWS, seq)
    mod_all = _mod_call(c, w_ada.astype(F32), b_ada).reshape(depth, bsz, 3, d)
    lower = _lower_bound_call(hg_lb_logits)

    def rowvec(v):
        return v.astype(F32)[:, None, :]

    def pad_rowvec(v):
        return jnp.pad(v.astype(F32), ((0, 0), (0, DT_PAD - v.shape[-1])))[:, None, :]

    col_scale = jnp.concatenate([
        jnp.full((LRU_W,), 1.0), jnp.full((LRU_W,), 0.5),
        jnp.full((2 * HG_W,), 0.5), jnp.full((HG_W,), 1.0), jnp.full((HG_W,), 0.5),
        jnp.full((SSD_W,), 0.5), jnp.full((SSD_XBC,), 1.0)]).astype(F32)
    w_main = _repack_call(w_in, col_scale.reshape(1, N_MAIN))
    w_dt = jnp.pad(w_in[:, :, N_MAIN:], ((0, 0), (0, 0), (0, DT_PAD - SSD_HEADS))).astype(BF16)
    hpb = LRU_HEADS // 2
    w_gate = jnp.stack([
        jnp.concatenate([_block_diag(lru_wa[:, hf * hpb:(hf + 1) * hpb]),
                         _block_diag(lru_wx[:, hf * hpb:(hf + 1) * hpb])], axis=-1)
        for hf in range(2)], axis=1).astype(BF16)
    stacked = (
        rowvec(norm_w), w_main, w_dt,
        0.5 * lru_conv_w.astype(F32), rowvec(0.5 * lru_conv_b), w_gate,
        rowvec(0.5 * lru_ba.reshape(depth, -1)), rowvec(0.5 * lru_bx.reshape(depth, -1)),
        rowvec(lru_lambda),
        rowvec(lower), rowvec(hg_norm_w),
        0.5 * ssd_conv_w.astype(F32), rowvec(0.5 * ssd_conv_b),
        pad_rowvec(ssd_dt_bias), pad_rowvec(ssd_a_log),
        rowvec(jnp.repeat(ssd_d, SSD_P, axis=-1)), rowvec(ssd_norm_w),
        w_out.astype(BF16),
    )
    head_of_lane = jnp.arange(SSD_W) // SSD_P
    consts = (
        final_norm_w.astype(F32).reshape(1, d),
        _chunk_tril(rows, HG_CHUNK), _chunk_tril(rows, SSD_CHUNK),
        (jnp.arange(2 * DT_PAD)[:, None] % DT_PAD == head_of_lane[None, :]).astype(BF16),
    )
    def trunk(x0, *, exact):
        unsafe = jnp.zeros((1,), jnp.int32)
        for l in range(depth):
            x0, flag = _layer_call(x0, mod_all, stacked, consts, l, final=(l == depth - 1),
                                   exact=exact)
            unsafe = jnp.maximum(unsafe, flag)
        return x0, unsafe

    out, unsafe = trunk(x, exact=False)
    return lax.cond(unsafe[0] > 0,
                    lambda x_in, x_out: trunk(x_in, exact=True)[0],
                    lambda x_in, x_out: x_out,
                    x, out)
```
